```python
import jax, jax.numpy as jnp
from jax import lax
import numpy as np

D_MODEL = 2048
BATCH = 4
SEQ = 4096
DEPTH = 1

A_HEAD_DIM = 128
A_WIDTH = D_MODEL // 2
A_HEADS = A_WIDTH // A_HEAD_DIM
A_CHUNK = 64
B_HEAD_DIM = 64
B_WIDTH = D_MODEL // 2
B_Q_HEADS = B_WIDTH // B_HEAD_DIM
B_GROUP = 4
B_KV_HEADS = B_Q_HEADS // B_GROUP
B_KV_WIDTH = B_KV_HEADS * B_HEAD_DIM
WINDOW = 128
BLOCK = 128
MLP_HIDDEN = 4 * D_MODEL
N_MOD = 6
EPS = 1e-6

SPLIT_SIZES = (A_WIDTH, A_WIDTH, A_WIDTH, A_WIDTH,
               B_WIDTH, B_KV_WIDTH, B_KV_WIDTH,
               D_MODEL, D_MODEL)
IN_WIDTH = 4 * A_WIDTH + B_WIDTH + 2 * B_KV_WIDTH + 2 * D_MODEL

kernel_name = "hybrid_hgrn2_swa_sink_gated_block"


def split_columns(t):
    idx, acc = [], 0
    for s in SPLIT_SIZES[:-1]:
        acc += s
        idx.append(acc)
    return jnp.split(t, idx, axis=-1)


def rms_norm(x, gain):
    xf = x.astype(jnp.float32)
    y = xf * lax.rsqrt(jnp.mean(xf * xf, axis=-1, keepdims=True) + EPS)
    return (y * gain.astype(jnp.float32)).astype(x.dtype)


def head_rms(t, gain):
    return t * lax.rsqrt(jnp.mean(t * t, axis=-1, keepdims=True) + EPS) * gain.astype(jnp.float32)


def hgrn2_mixer(q, f_logit, i, g, lb, o_gain):
    f32 = jnp.float32
    bsz, seq, _ = q.shape
    H, K, C = A_HEADS, A_HEAD_DIM, A_CHUNK
    n = seq // C
    lbf = lb.astype(f32)
    f = lbf + (1.0 - lbf) * jax.nn.sigmoid(f_logit.astype(f32))
    log_f = jnp.log(f)
    k = 1.0 - f
    qf = jax.nn.silu(q.astype(f32))

    def to_chunks(t):
        return t.reshape(bsz, n, C, H, K).transpose(0, 3, 1, 2, 4)

    qc, kc, vc, lfc = (to_chunks(t) for t in (qf, k, i.astype(f32), log_f))
    b = jnp.cumsum(lfc, axis=3)
    b_mid = b[:, :, :, C // 2 - 1:C // 2, :]
    b_last = b[:, :, :, C - 1:C, :]
    q_dec = qc * jnp.exp(b - b_mid)
    k_dec = kc * jnp.exp(b_mid - b)
    causal = jnp.tril(jnp.ones((C, C), dtype=bool))
    scores = jnp.where(causal, jnp.einsum('bhntk,bhnsk->bhnts', q_dec, k_dec), 0.0)
    o_intra = jnp.einsum('bhnts,bhnsv->bhntv', scores, vc)
    d_state = jnp.einsum('bhnsk,bhnsv->bhnkv', kc * jnp.exp(b_last - b), vc)
    chunk_decay = jnp.exp(b_last[:, :, :, 0, :])

    def step(state, inp):
        ds, dec = inp
        return dec[..., None] * state + ds, state

    s0 = jnp.zeros((bsz, H, K, K), f32)
    _, s_prev = lax.scan(step, s0, (jnp.moveaxis(d_state, 2, 0), jnp.moveaxis(chunk_decay, 2, 0)))
    s_prev = jnp.moveaxis(s_prev, 0, 2)
    o_inter = jnp.einsum('bhntk,bhnkv->bhntv', qc * jnp.exp(b), s_prev)
    o = (o_intra + o_inter).transpose(0, 2, 3, 1, 4).reshape(bsz, seq, H, K)
    o = head_rms(o, o_gain.reshape(H, K)).reshape(bsz, seq, A_WIDTH)
    o = o * jax.nn.silu(g.astype(f32))
    return o.astype(q.dtype)


def swa_sink_attention(q, k, v, q_gain, k_gain, sinks):
    f32 = jnp.float32
    bsz, seq, _ = q.shape
    nb = seq // BLOCK
    qh = head_rms(q.astype(f32).reshape(bsz, seq, B_KV_HEADS, B_GROUP, B_HEAD_DIM), q_gain)
    kh = head_rms(k.astype(f32).reshape(bsz, seq, B_KV_HEADS, B_HEAD_DIM), k_gain)
    vh = v.astype(f32).reshape(bsz, seq, B_KV_HEADS, B_HEAD_DIM)
    qb = qh.reshape(bsz, nb, BLOCK, B_KV_HEADS, B_GROUP, B_HEAD_DIM)
    kb = kh.reshape(bsz, nb, BLOCK, B_KV_HEADS, B_HEAD_DIM)
    vb = vh.reshape(bsz, nb, BLOCK, B_KV_HEADS, B_HEAD_DIM)

    def with_prev(t):
        prev = jnp.concatenate([jnp.zeros_like(t[:, :1]), t[:, :-1]], axis=1)
        return jnp.concatenate([prev, t], axis=2)

    kw, vw = with_prev(kb), with_prev(vb)
    scale = B_HEAD_DIM ** -0.5
    scores = jnp.einsum('bnqhgd,bnkhd->bnhgqk', qb, kw) * scale
    qi = jnp.arange(BLOCK)[:, None] + BLOCK
    ki = jnp.arange(2 * BLOCK)[None, :]
    rel = qi - ki
    band = (rel >= 0) & (rel < WINDOW)
    has_key = (jnp.arange(nb) > 0)[:, None, None] | (ki >= BLOCK)[None]
    mask = band[None] & has_key
    scores = jnp.where(mask[None, :, None, None], scores, -jnp.inf)
    sink = jnp.broadcast_to(sinks.astype(f32).reshape(B_KV_HEADS, B_GROUP)[None, None, :, :, None, None],
                            scores.shape[:-1] + (1,))
    probs = jax.nn.softmax(jnp.concatenate([scores, sink], axis=-1), axis=-1)[..., :-1]
    out = jnp.einsum('bnhgqk,bnkhd->bnqhgd', probs, vw)
    return out.reshape(bsz, seq, B_WIDTH).astype(q.dtype)


def setup_inputs(seed: int = 0) -> dict:
    key = jax.random.key(seed)
    ks = jax.random.split(key, 20)
    f32 = jnp.float32

    def w(k, shape, fan_in):
        return jax.random.normal(k, shape, f32) * (fan_in ** -0.5)

    return {
        "x": jax.random.normal(ks[0], (BATCH, SEQ, D_MODEL), f32),
        "c": jax.random.normal(ks[1], (BATCH, D_MODEL), f32),
        "w_ada": w(ks[2], (DEPTH, D_MODEL, N_MOD * D_MODEL), D_MODEL),
        "b_ada": 0.02 * jax.random.normal(ks[3], (DEPTH, N_MOD * D_MODEL), f32),
        "norm1_gain": 1.0 + 0.02 * jax.random.normal(ks[4], (DEPTH, D_MODEL), f32),
        "w_in": w(ks[5], (DEPTH, D_MODEL, IN_WIDTH), D_MODEL),
        "lb_logits": 0.5 * jax.random.normal(ks[6], (DEPTH + 1, A_WIDTH), f32),
        "hgrn_o_gain": 1.0 + 0.02 * jax.random.normal(ks[7], (DEPTH, A_WIDTH), f32),
        "q_norm_gain": 1.0 + 0.02 * jax.random.normal(ks[8], (DEPTH, B_HEAD_DIM), f32),
        "k_norm_gain": 1.0 + 0.02 * jax.random.normal(ks[9], (DEPTH, B_HEAD_DIM), f32),
        "sinks": 0.5 * jax.random.normal(ks[10], (DEPTH, B_Q_HEADS), f32),
        "w_branch_a": w(ks[11], (DEPTH, A_WIDTH, D_MODEL), A_WIDTH),
        "w_branch_b": w(ks[12], (DEPTH, B_WIDTH, D_MODEL), B_WIDTH),
        "w_out": w(ks[13], (DEPTH, D_MODEL, D_MODEL), D_MODEL),
        "norm2_gain": 1.0 + 0.02 * jax.random.normal(ks[14], (DEPTH, D_MODEL), f32),
        "w_mlp_in": w(ks[15], (DEPTH, D_MODEL, MLP_HIDDEN), D_MODEL),
        "w_mlp_out": w(ks[16], (DEPTH, MLP_HIDDEN, D_MODEL), MLP_HIDDEN),
    }


def reference(x, c, w_ada, b_ada, norm1_gain, w_in, lb_logits, hgrn_o_gain, q_norm_gain,
              k_norm_gain, sinks, w_branch_a, w_branch_b, w_out, norm2_gain, w_mlp_in, w_mlp_out):
    lb_all = jnp.cumsum(jax.nn.softmax(lb_logits.astype(jnp.float32), axis=0), axis=0)
    for l in range(DEPTH):
        mod = jax.nn.silu(c) @ w_ada[l] + b_ada[l]
        sh1, sc1, gt1, sh2, sc2, gt2 = (m[:, None, :] for m in jnp.split(mod, N_MOD, axis=-1))
        h = rms_norm(x, norm1_gain[l]) * (1.0 + sc1) + sh1
        qa, fa, ia, ga, qb, kb, vb, gate_a, gate_b = split_columns(h @ w_in[l])
        ya = hgrn2_mixer(qa, fa, ia, ga, lb_all[l], hgrn_o_gain[l]) @ w_branch_a[l]
        yb = swa_sink_attention(qb, kb, vb, q_norm_gain[l], k_norm_gain[l], sinks[l]) @ w_branch_b[l]
        merged = jax.nn.sigmoid(gate_a) * ya + jax.nn.sigmoid(gate_b) * yb
        x = x + gt1 * (merged @ w_out[l])
        h2 = rms_norm(x, norm2_gain[l]) * (1.0 + sc2) + sh2
        x = x + gt2 * (jnp.square(jax.nn.relu(h2 @ w_mlp_in[l])) @ w_mlp_out[l])
    return x
```

```python
import functools

import jax
import jax.numpy as jnp
from jax import lax
from jax.experimental import pallas as pl
from jax.experimental.pallas import tpu as pltpu

F32 = jnp.float32
BF16 = jnp.bfloat16

EPS = 1e-6
N_MOD = 6
A_HEAD_DIM = 128
A_CHUNK = 64
B_HEAD_DIM = 64
B_GROUP = 4
WINDOW = 128
LANES = 128
V7X_VMEM_BYTES = 64 * 1024 * 1024


def _vmem_limit(estimate_bytes):
    return int(min(estimate_bytes * 5 // 4 + (4 << 20), V7X_VMEM_BYTES - (6 << 20)))


def _tile(n, pref):
    t = min(n, pref)
    while n % t:
        t -= 1
    return t


def _nt_dot(a, b):
    return lax.dot_general(a, b, (((1,), (1,)), ((), ())), preferred_element_type=F32)


def _tn_dot(a, b):
    return lax.dot_general(a, b, (((0,), (0,)), ((), ())), preferred_element_type=F32)


def _dot(a, b):
    return jnp.dot(a, b, preferred_element_type=F32)


def _silu(x):
    return x * jax.nn.sigmoid(x)


def _ada_kernel(c_ref, w_ref, b_ref, o_ref):
    s = _silu(c_ref[...]).astype(BF16)
    o_ref[...] = _dot(s, w_ref[...].astype(BF16)) + b_ref[...]


def _ada(c, w, b):
    bsz, d = c.shape
    n = w.shape[1]
    rows = -(-bsz // 8) * 8
    cp = jnp.pad(c, ((0, rows - bsz), (0, 0)))
    tn = _tile(n, 1024)
    est = 2 * (d * tn * 4) + d * tn * 2 + 4 * rows * (d + 2 * tn) * 4
    out = pl.pallas_call(
        _ada_kernel,
        grid=(n // tn,),
        in_specs=[pl.BlockSpec((rows, d), lambda j: (0, 0)),
                  pl.BlockSpec((d, tn), lambda j: (0, j)),
                  pl.BlockSpec((1, tn), lambda j: (0, j))],
        out_specs=pl.BlockSpec((rows, tn), lambda j: (0, j)),
        out_shape=jax.ShapeDtypeStruct((rows, n), F32),
        compiler_params=pltpu.CompilerParams(dimension_semantics=("arbitrary",),
                                             vmem_limit_bytes=_vmem_limit(est)),
        name="ada_mod",
    )(cp, w, b.reshape(1, n))
    return out[:bsz]


def _mod_norm_rows(x, gain, scale, shift):
    y = x * lax.rsqrt(jnp.mean(x * x, axis=-1, keepdims=True) + EPS) * gain
    return y * (1.0 + scale) + shift


def _norm_into(x_ref, gain_ref, scale_ref, shift_ref, h_ref, rows_per_iter):
    tm = x_ref.shape[0]
    gain = gain_ref[...]
    scale = scale_ref[0]
    shift = shift_ref[0]

    def body(r, carry):
        rs = pl.ds(pl.multiple_of(r * rows_per_iter, rows_per_iter), rows_per_iter)
        h_ref[rs, :] = _mod_norm_rows(x_ref[rs, :], gain, scale, shift).astype(h_ref.dtype)
        return carry

    lax.fori_loop(0, tm // rows_per_iter, body, 0)


def _inp_kernel(x_ref, gain_ref, scale_ref, shift_ref, w_ref, o_ref, h_ref, *, rows_per_iter):
    @pl.when(pl.program_id(1) == 0)
    def _():
        _norm_into(x_ref, gain_ref, scale_ref, shift_ref, h_ref, rows_per_iter)

    o_ref[...] = _dot(h_ref[...], w_ref[...]).astype(o_ref.dtype)


def _inp(x2, gain, scale, shift, w, seq, tm_pref=1024, tn_pref=512):
    t, d = x2.shape
    n = w.shape[1]
    tm = _tile(seq, tm_pref)
    tn = _tile(n, tn_pref)
    per_b = seq // tm
    est = 2 * tm * d * 4 + tm * d * 2 + 2 * d * tn * 2 + 2 * tm * tn * 2 + tm * tn * 4
    return pl.pallas_call(
        functools.partial(_inp_kernel, rows_per_iter=_tile(tm, 64)),
        grid=(t // tm, n // tn),
        in_specs=[pl.BlockSpec((tm, d), lambda i, j: (i, 0)),
                  pl.BlockSpec((1, d), lambda i, j: (0, 0)),
                  pl.BlockSpec((1, 1, d), lambda i, j: (i // per_b, 0, 0)),
                  pl.BlockSpec((1, 1, d), lambda i, j: (i // per_b, 0, 0)),
                  pl.BlockSpec((d, tn), lambda i, j: (0, j))],
        out_specs=pl.BlockSpec((tm, tn), lambda i, j: (i, j)),
        out_shape=jax.ShapeDtypeStruct((t, n), BF16),
        scratch_shapes=[pltpu.VMEM((tm, d), BF16)],
        compiler_params=pltpu.CompilerParams(dimension_semantics=("arbitrary", "arbitrary"),
                                             vmem_limit_bytes=_vmem_limit(est)),
        name="inp_proj",
    )(x2, gain, scale, shift, w)


def _split3(x):
    hi = x.astype(BF16)
    r1 = x - hi.astype(F32)
    mid = r1.astype(BF16)
    lo = (r1 - mid.astype(F32)).astype(BF16)
    return hi, mid, lo


def _hgrn_kernel(q_ref, f_ref, i_ref, g_ref, lbl_ref, gain_ref, o_ref, st_ref, *, layer, chunk, hd):
    tb, width = q_ref.shape
    n_heads = width // hd

    @pl.when(pl.program_id(1) == 0)
    def _():
        st_ref[...] = jnp.zeros_like(st_ref)

    lvl = lbl_ref[...]
    e = jnp.exp(lvl - jnp.max(lvl, axis=0, keepdims=True))
    lb = jnp.sum(e[:layer + 1], axis=0, keepdims=True) / jnp.sum(e, axis=0, keepdims=True)

    f = lb + (1.0 - lb) * jax.nn.sigmoid(f_ref[...].astype(F32))
    log_f = jnp.log(f)
    r = lax.broadcasted_iota(jnp.int32, (tb, tb), 0)
    c = lax.broadcasted_iota(jnp.int32, (tb, tb), 1)
    tri = ((c <= r) & (c >= (r & (-chunk)))).astype(BF16)
    hi, mid, lo = _split3(log_f)
    b_all = _dot(tri, hi) + _dot(tri, mid) + _dot(tri, lo)

    rr = lax.broadcasted_iota(jnp.int32, (chunk, chunk), 0)
    cc = lax.broadcasted_iota(jnp.int32, (chunk, chunk), 1)
    causal = cc <= rr
    gain = gain_ref[...]

    for ci in range(tb // chunk):
        rows = slice(ci * chunk, (ci + 1) * chunk)
        b = b_all[rows, :]
        b_mid = b[chunk // 2 - 1:chunk // 2, :]
        b_last = b[chunk - 1:chunk, :]
        qs = _silu(q_ref[rows, :].astype(F32))
        k = 1.0 - f[rows, :]
        q_dec = (qs * jnp.exp(b - b_mid)).astype(BF16)
        k_dec = (k * jnp.exp(b_mid - b)).astype(BF16)
        k_st = (k * jnp.exp(b_last - b)).astype(BF16)
        q_int = (qs * jnp.exp(b)).astype(BF16)
        decay = jnp.exp(b_last)
        v = i_ref[rows, :]
        gate = _silu(g_ref[rows, :].astype(F32))
        for h in range(n_heads):
            cols = slice(h * hd, (h + 1) * hd)
            scores = jnp.where(causal, _nt_dot(q_dec[:, cols], k_dec[:, cols]), 0.0).astype(BF16)
            st = st_ref[h]
            o = _dot(scores, v[:, cols]) + _nt_dot(q_int[:, cols], st.astype(BF16))
            st_ref[h] = st * decay[:, cols] + _tn_dot(v[:, cols], k_st[:, cols])
            o = o * lax.rsqrt(jnp.mean(o * o, axis=-1, keepdims=True) + EPS) * gain[:, cols]
            o_ref[rows, cols] = (o * gate[:, cols]).astype(o_ref.dtype)


def _hgrn(proj, col0, width, lb_logits, o_gain, bsz, seq, layer, tb_pref=256):
    t = proj.shape[0]
    tb = _tile(seq, tb_pref)
    per_b = seq // tb
    n_heads = width // A_HEAD_DIM
    cb = col0 // width

    def spec(k):
        return pl.BlockSpec((tb, width), lambda b, s: (b * per_b + s, cb + k))

    est = 2 * 5 * tb * width * 2 + n_heads * A_HEAD_DIM * A_HEAD_DIM * 4 + 24 * tb * width * 4
    return pl.pallas_call(
        functools.partial(_hgrn_kernel, layer=layer, chunk=A_CHUNK, hd=A_HEAD_DIM),
        grid=(bsz, per_b),
        in_specs=[spec(0), spec(1), spec(2), spec(3),
                  pl.BlockSpec(lb_logits.shape, lambda b, s: (0, 0)),
                  pl.BlockSpec((1, width), lambda b, s: (0, 0))],
        out_specs=pl.BlockSpec((tb, width), lambda b, s: (b * per_b + s, 0)),
        out_shape=jax.ShapeDtypeStruct((t, width), BF16),
        scratch_shapes=[pltpu.VMEM((n_heads, A_HEAD_DIM, A_HEAD_DIM), F32)],
        compiler_params=pltpu.CompilerParams(dimension_semantics=("arbitrary", "arbitrary"),
                                             vmem_limit_bytes=_vmem_limit(est)),
        name="hgrn2",
    )(proj, proj, proj, proj, lb_logits, o_gain)


def _group_mean_sq(x, ones_blk):
    x2 = x * x
    hi = x2.astype(BF16)
    lo = (x2 - hi.astype(F32)).astype(BF16)
    parts = []
    for g in range(x.shape[1] // LANES):
        cols = slice(g * LANES, (g + 1) * LANES)
        parts.append(_dot(hi[:, cols], ones_blk) + _dot(lo[:, cols], ones_blk))
    return jnp.concatenate(parts, axis=1) * (1.0 / B_HEAD_DIM)


def _swa_kernel(sink_ref, q_ref, kc_ref, vc_ref, kp_ref, vp_ref, qg_ref, kg_ref, o_ref):
    blk = q_ref.shape[0]
    n_pairs = q_ref.shape[1] // LANES
    has_prev = pl.program_id(1) > 0

    lane = lax.broadcasted_iota(jnp.int32, (1, LANES), 1)
    lo_half = lane < B_HEAD_DIM
    r = lax.broadcasted_iota(jnp.int32, (LANES, LANES), 0)
    c = lax.broadcasted_iota(jnp.int32, (LANES, LANES), 1)
    ones_blk = ((r < B_HEAD_DIM) == (c < B_HEAD_DIM)).astype(BF16)

    q = q_ref[...].astype(F32)
    qn = q * lax.rsqrt(_group_mean_sq(q, ones_blk) + EPS) * (qg_ref[...] * (B_HEAD_DIM ** -0.5))
    kcat = jnp.concatenate([kp_ref[...], kc_ref[...]], axis=0).astype(F32)
    kn = kcat * lax.rsqrt(_group_mean_sq(kcat, ones_blk) + EPS) * kg_ref[...]
    vcat = jnp.concatenate([vp_ref[...], vc_ref[...]], axis=0).astype(F32)

    qi = lax.broadcasted_iota(jnp.int32, (blk, 2 * blk), 0) + blk
    ki = lax.broadcasted_iota(jnp.int32, (blk, 2 * blk), 1)
    rel = qi - ki
    valid = (rel >= 0) & (rel < WINDOW) & ((ki >= blk) | has_prev)

    def both_halves(x, take_lo):
        sw = pltpu.roll(x, B_HEAD_DIM, 1)
        return jnp.where(lo_half == take_lo, x, sw)

    for p in range(n_pairs):
        hk = (2 * p) // B_GROUP
        grp = slice((hk // 2) * LANES, (hk // 2 + 1) * LANES)
        take_lo = (hk % 2 == 0)
        kk = both_halves(kn[:, grp], take_lo).astype(BF16)
        vv = both_halves(vcat[:, grp], take_lo)
        v_lo = jnp.where(lo_half, vv, 0.0).astype(BF16)
        v_hi = jnp.where(lo_half, 0.0, vv).astype(BF16)
        qp = qn[:, p * LANES:(p + 1) * LANES]
        probs = []
        for half, q_half in enumerate((jnp.where(lo_half, qp, 0.0), jnp.where(lo_half, 0.0, qp))):
            sink = sink_ref[2 * p + half]
            s = jnp.where(valid, _nt_dot(q_half.astype(BF16), kk), -jnp.inf)
            m = jnp.maximum(jnp.max(s, axis=-1, keepdims=True), sink)
            e = jnp.exp(s - m)
            denom = jnp.sum(e, axis=-1, keepdims=True) + jnp.exp(sink - m)
            probs.append((e / denom).astype(BF16))
        o_ref[:, p * LANES:(p + 1) * LANES] = (_dot(probs[0], v_lo) + _dot(probs[1], v_hi)).astype(o_ref.dtype)


def _swa(proj, q_col0, k_col0, v_col0, q_gain, k_gain, sinks, bsz, seq):
    t = proj.shape[0]
    blk = WINDOW
    nb = seq // blk
    qw = sinks.shape[0] * B_HEAD_DIM
    kvw = qw // B_GROUP
    qg = jnp.tile(q_gain, qw // B_HEAD_DIM).reshape(1, qw)
    kg = jnp.tile(k_gain, kvw // B_HEAD_DIM).reshape(1, kvw)

    def cur(col0, w):
        return pl.BlockSpec((blk, w), lambda b, n: (b * nb + n, col0 // w))

    def prev(col0, w):
        return pl.BlockSpec((blk, w), lambda b, n: (b * nb + jnp.maximum(n - 1, 0), col0 // w))

    est = 2 * (2 * blk * qw * 2 + 4 * blk * kvw * 2) + 16 * blk * qw * 4
    return pl.pallas_call(
        _swa_kernel,
        grid=(bsz, nb),
        in_specs=[pl.BlockSpec(memory_space=pltpu.SMEM),
                  cur(q_col0, qw), cur(k_col0, kvw), cur(v_col0, kvw), prev(k_col0, kvw), prev(v_col0, kvw),
                  pl.BlockSpec((1, qw), lambda b, n: (0, 0)),
                  pl.BlockSpec((1, kvw), lambda b, n: (0, 0))],
        out_specs=pl.BlockSpec((blk, qw), lambda b, n: (b * nb + n, 0)),
        out_shape=jax.ShapeDtypeStruct((t, qw), BF16),
        compiler_params=pltpu.CompilerParams(dimension_semantics=("arbitrary", "arbitrary"),
                                             vmem_limit_bytes=_vmem_limit(est)),
        name="swa_sink",
    )(sinks, proj, proj, proj, proj, proj, qg, kg)


def _mix_kernel(oa_ref, ob_ref, ga_ref, gb_ref, x_ref, gt_ref, wa_ref, wb_ref, wo_ref, o_ref):
    ya = _dot(oa_ref[...], wa_ref[...])
    yb = _dot(ob_ref[...], wb_ref[...])
    merged = jax.nn.sigmoid(ga_ref[...].astype(F32)) * ya + jax.nn.sigmoid(gb_ref[...].astype(F32)) * yb
    o_ref[...] = x_ref[...] + gt_ref[0] * _dot(merged.astype(BF16), wo_ref[...])


def _mix(oa, ob, proj, ga_col0, gb_col0, x2, gate, wa, wb, wo, seq, tm_pref=512):
    t, d = x2.shape
    tm = _tile(seq, tm_pref)
    per_b = seq // tm
    wa_w, wb_w = wa.shape[0], wb.shape[0]
    est = (2 * tm * (wa_w + wb_w + 2 * d) * 2 + 4 * tm * d * 4 + (wa_w + wb_w + d) * d * 2 + 4 * tm * d * 4)
    resident = dict(pipeline_mode=pl.Buffered(1))
    return pl.pallas_call(
        _mix_kernel,
        grid=(t // tm,),
        in_specs=[pl.BlockSpec((tm, wa_w), lambda i: (i, 0)),
                  pl.BlockSpec((tm, wb_w), lambda i: (i, 0)),
                  pl.BlockSpec((tm, d), lambda i: (i, ga_col0 // d)),
                  pl.BlockSpec((tm, d), lambda i: (i, gb_col0 // d)),
                  pl.BlockSpec((tm, d), lambda i: (i, 0)),
                  pl.BlockSpec((1, 1, d), lambda i: (i // per_b, 0, 0)),
                  pl.BlockSpec((wa_w, d), lambda i: (0, 0), **resident),
                  pl.BlockSpec((wb_w, d), lambda i: (0, 0), **resident),
                  pl.BlockSpec((d, d), lambda i: (0, 0), **resident)],
        out_specs=pl.BlockSpec((tm, d), lambda i: (i, 0)),
        out_shape=jax.ShapeDtypeStruct((t, d), F32),
        compiler_params=pltpu.CompilerParams(dimension_semantics=("arbitrary",),
                                             vmem_limit_bytes=_vmem_limit(est)),
        name="mix_out",
    )(oa, ob, proj, proj, x2, gate, wa, wb, wo)


def _mlp_kernel(x_ref, gain_ref, scale_ref, shift_ref, gt_ref, w1_ref, w2_ref, o_ref, h_ref, acc_ref,
                *, rows_per_iter):
    k = pl.program_id(1)

    @pl.when(k == 0)
    def _():
        _norm_into(x_ref, gain_ref, scale_ref, shift_ref, h_ref, rows_per_iter)

    hid = jnp.maximum(_dot(h_ref[...], w1_ref[...]), 0.0)
    contrib = _dot((hid * hid).astype(BF16), w2_ref[...])

    @pl.when(k == 0)
    def _():
        acc_ref[...] = contrib

    @pl.when(k > 0)
    def _():
        acc_ref[...] += contrib

    @pl.when(k == pl.num_programs(1) - 1)
    def _():
        o_ref[...] = x_ref[...] + gt_ref[0] * acc_ref[...]


def _mlp(x2, gain, scale, shift, gate, w1, w2, seq, tm_pref=512, th_pref=512):
    t, d = x2.shape
    hidden = w1.shape[1]
    tm = _tile(seq, tm_pref)
    th = _tile(hidden, th_pref)
    per_b = seq // tm
    est = 4 * tm * d * 4 + tm * d * 2 + tm * d * 4 + 4 * d * th * 2 + tm * th * 6 + tm * d * 4

    def vec():
        return pl.BlockSpec((1, 1, d), lambda i, k: (i // per_b, 0, 0))

    return pl.pallas_call(
        functools.partial(_mlp_kernel, rows_per_iter=_tile(tm, 64)),
        grid=(t // tm, hidden // th),
        in_specs=[pl.BlockSpec((tm, d), lambda i, k: (i, 0)),
                  pl.BlockSpec((1, d), lambda i, k: (0, 0)),
                  vec(), vec(), vec(),
                  pl.BlockSpec((d, th), lambda i, k: (0, k)),
                  pl.BlockSpec((th, d), lambda i, k: (k, 0))],
        out_specs=pl.BlockSpec((tm, d), lambda i, k: (i, 0)),
        out_shape=jax.ShapeDtypeStruct((t, d), F32),
        scratch_shapes=[pltpu.VMEM((tm, d), BF16), pltpu.VMEM((tm, d), F32)],
        compiler_params=pltpu.CompilerParams(dimension_semantics=("arbitrary", "arbitrary"),
                                             vmem_limit_bytes=_vmem_limit(est)),
        name="mlp_relu2",
    )(x2, gain, scale, shift, gate, w1, w2)


def kernel(x, c, w_ada, b_ada, norm1_gain, w_in, lb_logits, hgrn_o_gain, q_norm_gain, k_norm_gain, sinks,
           w_branch_a, w_branch_b, w_out, norm2_gain, w_mlp_in, w_mlp_out):
    bsz, seq, d = x.shape
    depth = w_ada.shape[0]
    a_w = w_branch_a.shape[1]
    b_w = w_branch_b.shape[1]
    kv_w = b_w // B_GROUP
    a0, qb0 = 0, 4 * a_w
    kb0, vb0 = qb0 + b_w, qb0 + b_w + kv_w
    ga0, gb0 = vb0 + kv_w, vb0 + kv_w + d
    n_ga0, n_gb0, n_a0, n_qb0 = 0, d, 2 * d, 2 * d + 4 * a_w
    n_kb0, n_vb0 = n_qb0 + b_w, n_qb0 + b_w + kv_w

    x2 = x.reshape(bsz * seq, d)
    for l in range(depth):
        mod = _ada(c, w_ada[l], b_ada[l])
        sh1, sc1, gt1, sh2, sc2, gt2 = (m.reshape(bsz, 1, d) for m in jnp.split(mod, N_MOD, axis=-1))
        wi = w_in[l]
        w_in_p = jnp.concatenate([wi[:, ga0:ga0 + d], wi[:, gb0:gb0 + d], wi[:, a0:a0 + 4 * a_w],
                                  wi[:, qb0:qb0 + b_w], wi[:, kb0:kb0 + kv_w], wi[:, vb0:vb0 + kv_w]],
                                 axis=1).astype(BF16)
        proj = _inp(x2, norm1_gain[l].reshape(1, d), sc1, sh1, w_in_p, seq)
        oa = _hgrn(proj, n_a0, a_w, lb_logits, hgrn_o_gain[l].reshape(1, a_w), bsz, seq, l)
        ob = _swa(proj, n_qb0, n_kb0, n_vb0, q_norm_gain[l], k_norm_gain[l], sinks[l], bsz, seq)
        x2 = _mix(oa, ob, proj, n_ga0, n_gb0, x2, gt1, w_branch_a[l].astype(BF16), w_branch_b[l].astype(BF16),
                  w_out[l].astype(BF16), seq)
        x2 = _mlp(x2, norm2_gain[l].reshape(1, d), sc2, sh2, gt2, w_mlp_in[l].astype(BF16),
                  w_mlp_out[l].astype(BF16), seq)
    return x2.reshape(bsz, seq, d)
```

```python
import functools

import jax
import jax.numpy as jnp
from jax import lax
from jax.experimental import pallas as pl
from jax.experimental.pallas import tpu as pltpu

F32 = jnp.float32
BF16 = jnp.bfloat16

EPS = 1e-6
N_MOD = 6
A_HEAD_DIM = 128
A_CHUNK = 64
B_HEAD_DIM = 64
B_GROUP = 4
WINDOW = 128
LANES = 128
V7X_VMEM_BYTES = 64 * 1024 * 1024


def _vmem_limit(estimate_bytes):
    return int(min(estimate_bytes * 5 // 4 + (4 << 20), V7X_VMEM_BYTES - (6 << 20)))


def _tile(n, pref):
    t = min(n, pref)
    while n % t:
        t -= 1
    return t


def _nt_dot(a, b):
    return lax.dot_general(a, b, (((1,), (1,)), ((), ())), preferred_element_type=F32)


def _tn_dot(a, b):
    return lax.dot_general(a, b, (((0,), (0,)), ((), ())), preferred_element_type=F32)


def _dot(a, b):
    return jnp.dot(a, b, preferred_element_type=F32)


def _silu(x):
    return x * jax.nn.sigmoid(x)


def _ada_kernel(c_ref, w_ref, b_ref, o_ref):
    s = _silu(c_ref[...]).astype(BF16)
    o_ref[...] = _dot(s, w_ref[...].astype(BF16)) + b_ref[...]


def _ada(c, w, b):
    bsz, d = c.shape
    n = w.shape[1]
    rows = -(-bsz // 8) * 8
    cp = jnp.pad(c, ((0, rows - bsz), (0, 0)))
    tn = _tile(n, 1024)
    est = 2 * (d * tn * 4) + d * tn * 2 + 4 * rows * (d + 2 * tn) * 4
    out = pl.pallas_call(
        _ada_kernel,
        grid=(n // tn,),
        in_specs=[pl.BlockSpec((rows, d), lambda j: (0, 0)),
                  pl.BlockSpec((d, tn), lambda j: (0, j)),
                  pl.BlockSpec((1, tn), lambda j: (0, j))],
        out_specs=pl.BlockSpec((rows, tn), lambda j: (0, j)),
        out_shape=jax.ShapeDtypeStruct((rows, n), F32),
        compiler_params=pltpu.CompilerParams(dimension_semantics=("arbitrary",),
                                             vmem_limit_bytes=_vmem_limit(est)),
        name="ada_mod",
    )(cp, w, b.reshape(1, n))
    return out[:bsz]


def _mod_norm_rows(x, gain, scale, shift):
    y = x * lax.rsqrt(jnp.mean(x * x, axis=-1, keepdims=True) + EPS) * gain
    return y * (1.0 + scale) + shift


def _norm_into(x_ref, gain_ref, scale_ref, shift_ref, h_ref, rows_per_iter):
    tm = x_ref.shape[0]
    gain = gain_ref[...]
    scale = scale_ref[0]
    shift = shift_ref[0]

    def body(r, carry):
        rs = pl.ds(pl.multiple_of(r * rows_per_iter, rows_per_iter), rows_per_iter)
        h_ref[rs, :] = _mod_norm_rows(x_ref[rs, :], gain, scale, shift).astype(h_ref.dtype)
        return carry

    lax.fori_loop(0, tm // rows_per_iter, body, 0)


def _inp_kernel(x_ref, gain_ref, scale_ref, shift_ref, w_ref, o_ref, h_ref, *, rows_per_iter):
    @pl.when(pl.program_id(1) == 0)
    def _():
        _norm_into(x_ref, gain_ref, scale_ref, shift_ref, h_ref, rows_per_iter)

    o_ref[...] = _dot(h_ref[...], w_ref[...]).astype(o_ref.dtype)


def _inp(x2, gain, scale, shift, w, seq, tm_pref=1024, tn_pref=2432):
    t, d = x2.shape
    n = w.shape[1]
    tm = _tile(seq, tm_pref)
    tn = _tile(n, tn_pref)
    per_b = seq // tm
    est = 2 * tm * d * 4 + tm * d * 2 + 2 * d * tn * 2 + 2 * tm * tn * 2 + tm * tn * 4
    return pl.pallas_call(
        functools.partial(_inp_kernel, rows_per_iter=_tile(tm, 64)),
        grid=(t // tm, n // tn),
        in_specs=[pl.BlockSpec((tm, d), lambda i, j: (i, 0)),
                  pl.BlockSpec((1, d), lambda i, j: (0, 0)),
                  pl.BlockSpec((1, 1, d), lambda i, j: (i // per_b, 0, 0)),
                  pl.BlockSpec((1, 1, d), lambda i, j: (i // per_b, 0, 0)),
                  pl.BlockSpec((d, tn), lambda i, j: (0, j))],
        out_specs=pl.BlockSpec((tm, tn), lambda i, j: (i, j)),
        out_shape=jax.ShapeDtypeStruct((t, n), BF16),
        scratch_shapes=[pltpu.VMEM((tm, d), BF16)],
        compiler_params=pltpu.CompilerParams(dimension_semantics=("arbitrary", "arbitrary"),
                                             vmem_limit_bytes=_vmem_limit(est)),
        name="inp_proj",
    )(x2, gain, scale, shift, w)


def _split3(x):
    hi = x.astype(BF16)
    r1 = x - hi.astype(F32)
    mid = r1.astype(BF16)
    lo = (r1 - mid.astype(F32)).astype(BF16)
    return hi, mid, lo


def _hgrn_kernel(q_ref, f_ref, i_ref, g_ref, lbl_ref, gain_ref, o_ref, st_ref, *, layer, chunk, hd):
    tb, width = q_ref.shape
    n_heads = width // hd

    @pl.when(pl.program_id(1) == 0)
    def _():
        st_ref[...] = jnp.zeros_like(st_ref)

    lvl = lbl_ref[...]
    e = jnp.exp(lvl - jnp.max(lvl, axis=0, keepdims=True))
    lb = jnp.sum(e[:layer + 1], axis=0, keepdims=True) / jnp.sum(e, axis=0, keepdims=True)

    f = lb + (1.0 - lb) * jax.nn.sigmoid(f_ref[...].astype(F32))
    log_f = jnp.log(f)
    r = lax.broadcasted_iota(jnp.int32, (tb, tb), 0)
    c = lax.broadcasted_iota(jnp.int32, (tb, tb), 1)
    tri = ((c <= r) & (c >= (r & (-chunk)))).astype(BF16)
    hi, mid, lo = _split3(log_f)
    b_all = _dot(tri, hi) + _dot(tri, mid) + _dot(tri, lo)

    rr = lax.broadcasted_iota(jnp.int32, (chunk, chunk), 0)
    cc = lax.broadcasted_iota(jnp.int32, (chunk, chunk), 1)
    causal = cc <= rr
    gain = gain_ref[...]

    for ci in range(tb // chunk):
        rows = slice(ci * chunk, (ci + 1) * chunk)
        b = b_all[rows, :]
        b_mid = b[chunk // 2 - 1:chunk // 2, :]
        b_last = b[chunk - 1:chunk, :]
        qs = _silu(q_ref[rows, :].astype(F32))
        k = 1.0 - f[rows, :]
        q_dec = (qs * jnp.exp(b - b_mid)).astype(BF16)
        k_dec = (k * jnp.exp(b_mid - b)).astype(BF16)
        k_st = (k * jnp.exp(b_last - b)).astype(BF16)
        q_int = (qs * jnp.exp(b)).astype(BF16)
        decay = jnp.exp(b_last)
        v = i_ref[rows, :]
        gate = _silu(g_ref[rows, :].astype(F32))
        for h in range(n_heads):
            cols = slice(h * hd, (h + 1) * hd)
            scores = jnp.where(causal, _nt_dot(q_dec[:, cols], k_dec[:, cols]), 0.0).astype(BF16)
            st = st_ref[h]
            o = _dot(scores, v[:, cols]) + _nt_dot(q_int[:, cols], st.astype(BF16))
            st_ref[h] = st * decay[:, cols] + _tn_dot(v[:, cols], k_st[:, cols])
            o = o * lax.rsqrt(jnp.mean(o * o, axis=-1, keepdims=True) + EPS) * gain[:, cols]
            o_ref[rows, cols] = (o * gate[:, cols]).astype(o_ref.dtype)


def _hgrn(proj, col0, width, lb_logits, o_gain, bsz, seq, layer, tb_pref=256):
    t = proj.shape[0]
    tb = _tile(seq, tb_pref)
    per_b = seq // tb
    n_heads = width // A_HEAD_DIM
    cb = col0 // width

    def spec(k):
        return pl.BlockSpec((tb, width), lambda b, s: (b * per_b + s, cb + k))

    est = 2 * 5 * tb * width * 2 + n_heads * A_HEAD_DIM * A_HEAD_DIM * 4 + 24 * tb * width * 4
    return pl.pallas_call(
        functools.partial(_hgrn_kernel, layer=layer, chunk=A_CHUNK, hd=A_HEAD_DIM),
        grid=(bsz, per_b),
        in_specs=[spec(0), spec(1), spec(2), spec(3),
                  pl.BlockSpec(lb_logits.shape, lambda b, s: (0, 0)),
                  pl.BlockSpec((1, width), lambda b, s: (0, 0))],
        out_specs=pl.BlockSpec((tb, width), lambda b, s: (b * per_b + s, 0)),
        out_shape=jax.ShapeDtypeStruct((t, width), BF16),
        scratch_shapes=[pltpu.VMEM((n_heads, A_HEAD_DIM, A_HEAD_DIM), F32)],
        compiler_params=pltpu.CompilerParams(dimension_semantics=("arbitrary", "arbitrary"),
                                             vmem_limit_bytes=_vmem_limit(est)),
        name="hgrn2",
    )(proj, proj, proj, proj, lb_logits, o_gain)


def _group_mean_sq(x, ones_blk):
    x2 = x * x
    hi = x2.astype(BF16)
    lo = (x2 - hi.astype(F32)).astype(BF16)
    parts = []
    for g in range(x.shape[1] // LANES):
        cols = slice(g * LANES, (g + 1) * LANES)
        parts.append(_dot(hi[:, cols], ones_blk) + _dot(lo[:, cols], ones_blk))
    return jnp.concatenate(parts, axis=1) * (1.0 / B_HEAD_DIM)


def _swa_kernel(sink_ref, q_ref, kc_ref, vc_ref, kp_ref, vp_ref, qg_ref, kg_ref, o_ref):
    blk = q_ref.shape[0]
    n_pairs = q_ref.shape[1] // LANES
    has_prev = pl.program_id(1) > 0

    lane = lax.broadcasted_iota(jnp.int32, (1, LANES), 1)
    lo_half = lane < B_HEAD_DIM
    r = lax.broadcasted_iota(jnp.int32, (LANES, LANES), 0)
    c = lax.broadcasted_iota(jnp.int32, (LANES, LANES), 1)
    ones_blk = ((r < B_HEAD_DIM) == (c < B_HEAD_DIM)).astype(BF16)

    q = q_ref[...].astype(F32)
    qn = q * lax.rsqrt(_group_mean_sq(q, ones_blk) + EPS) * (qg_ref[...] * (B_HEAD_DIM ** -0.5))
    kcat = jnp.concatenate([kp_ref[...], kc_ref[...]], axis=0).astype(F32)
    kn = kcat * lax.rsqrt(_group_mean_sq(kcat, ones_blk) + EPS) * kg_ref[...]
    vcat = jnp.concatenate([vp_ref[...], vc_ref[...]], axis=0).astype(F32)

    ri = lax.broadcasted_iota(jnp.int32, (blk, blk), 0)
    ci = lax.broadcasted_iota(jnp.int32, (blk, blk), 1)
    in_cur = ci <= ri
    prev_bias = jnp.where(has_prev, 0.0, -jnp.inf)
    ones_cols = jnp.ones((2 * blk, LANES), BF16)
    pairs_per_kv = B_GROUP // 2

    def both_halves(x, in_lo):
        return jnp.where(lo_half == in_lo, x, pltpu.roll(x, B_HEAD_DIM, 1))

    for hk in range(n_pairs // pairs_per_kv):
        grp = slice((hk // 2) * LANES, (hk // 2 + 1) * LANES)
        in_lo = (hk % 2 == 0)
        keys = both_halves(kn[:, grp], in_lo).astype(BF16)
        vals = jnp.concatenate([both_halves(vcat[:, grp], in_lo).astype(BF16), ones_cols], axis=1)
        p0 = hk * pairs_per_kv
        q_rows = []
        for j in range(pairs_per_kv):
            qp = qn[:, (p0 + j) * LANES:(p0 + j + 1) * LANES]
            q_rows += [jnp.where(lo_half, qp, 0.0), jnp.where(lo_half, 0.0, qp)]
        s_all = _nt_dot(jnp.concatenate(q_rows, axis=0).astype(BF16), keys)
        e_rows, sink_terms = [], []
        for g in range(B_GROUP):
            sink = sink_ref[hk * B_GROUP + g]
            s_blk = s_all[g * blk:(g + 1) * blk]
            s = jnp.where(in_cur, s_blk[:, blk:], s_blk[:, :blk] + prev_bias)
            m = jnp.maximum(jnp.max(s, axis=-1, keepdims=True), sink)
            e = jnp.exp(s - m)
            e_rows.append(jnp.concatenate([jnp.where(in_cur, 0.0, e), jnp.where(in_cur, e, 0.0)], axis=1))
            sink_terms.append(jnp.exp(sink - m))
        pv = _dot(jnp.concatenate(e_rows, axis=0).astype(BF16), vals)
        for j in range(pairs_per_kv):
            heads = []
            for half in range(2):
                g = 2 * j + half
                rows = slice(g * blk, (g + 1) * blk)
                heads.append(pv[rows, :LANES] * (1.0 / (pv[rows, LANES:] + sink_terms[g])))
            o_ref[:, (p0 + j) * LANES:(p0 + j + 1) * LANES] = jnp.where(lo_half, heads[0], heads[1]).astype(o_ref.dtype)


def _swa(proj, q_col0, k_col0, v_col0, q_gain, k_gain, sinks, bsz, seq):
    t = proj.shape[0]
    blk = WINDOW
    nb = seq // blk
    qw = sinks.shape[0] * B_HEAD_DIM
    kvw = qw // B_GROUP
    qg = jnp.tile(q_gain, qw // B_HEAD_DIM).reshape(1, qw)
    kg = jnp.tile(k_gain, kvw // B_HEAD_DIM).reshape(1, kvw)

    def cur(col0, w):
        return pl.BlockSpec((blk, w), lambda b, n: (b * nb + n, col0 // w))

    def prev(col0, w):
        return pl.BlockSpec((blk, w), lambda b, n: (b * nb + jnp.maximum(n - 1, 0), col0 // w))

    est = 2 * (2 * blk * qw * 2 + 4 * blk * kvw * 2) + 16 * blk * qw * 4
    return pl.pallas_call(
        _swa_kernel,
        grid=(bsz, nb),
        in_specs=[pl.BlockSpec(memory_space=pltpu.SMEM),
                  cur(q_col0, qw), cur(k_col0, kvw), cur(v_col0, kvw), prev(k_col0, kvw), prev(v_col0, kvw),
                  pl.BlockSpec((1, qw), lambda b, n: (0, 0)),
                  pl.BlockSpec((1, kvw), lambda b, n: (0, 0))],
        out_specs=pl.BlockSpec((blk, qw), lambda b, n: (b * nb + n, 0)),
        out_shape=jax.ShapeDtypeStruct((t, qw), BF16),
        compiler_params=pltpu.CompilerParams(dimension_semantics=("arbitrary", "arbitrary"),
                                             vmem_limit_bytes=_vmem_limit(est)),
        name="swa_sink",
    )(sinks, proj, proj, proj, proj, proj, qg, kg)


def _mix_kernel(oa_ref, ob_ref, ga_ref, gb_ref, x_ref, gt_ref, wa_ref, wb_ref, wo_ref, o_ref):
    ya = _dot(oa_ref[...], wa_ref[...])
    yb = _dot(ob_ref[...], wb_ref[...])
    merged = jax.nn.sigmoid(ga_ref[...].astype(F32)) * ya + jax.nn.sigmoid(gb_ref[...].astype(F32)) * yb
    o_ref[...] = x_ref[...] + gt_ref[0] * _dot(merged.astype(BF16), wo_ref[...])


def _mix(oa, ob, proj, ga_col0, gb_col0, x2, gate, wa, wb, wo, seq, tm_pref=512):
    t, d = x2.shape
    tm = _tile(seq, tm_pref)
    per_b = seq // tm
    wa_w, wb_w = wa.shape[0], wb.shape[0]
    est = (2 * tm * (wa_w + wb_w + 2 * d) * 2 + 4 * tm * d * 4 + (wa_w + wb_w + d) * d * 2 + 4 * tm * d * 4)
    resident = dict(pipeline_mode=pl.Buffered(1))
    return pl.pallas_call(
        _mix_kernel,
        grid=(t // tm,),
        in_specs=[pl.BlockSpec((tm, wa_w), lambda i: (i, 0)),
                  pl.BlockSpec((tm, wb_w), lambda i: (i, 0)),
                  pl.BlockSpec((tm, d), lambda i: (i, ga_col0 // d)),
                  pl.BlockSpec((tm, d), lambda i: (i, gb_col0 // d)),
                  pl.BlockSpec((tm, d), lambda i: (i, 0)),
                  pl.BlockSpec((1, 1, d), lambda i: (i // per_b, 0, 0)),
                  pl.BlockSpec((wa_w, d), lambda i: (0, 0), **resident),
                  pl.BlockSpec((wb_w, d), lambda i: (0, 0), **resident),
                  pl.BlockSpec((d, d), lambda i: (0, 0), **resident)],
        out_specs=pl.BlockSpec((tm, d), lambda i: (i, 0)),
        out_shape=jax.ShapeDtypeStruct((t, d), F32),
        compiler_params=pltpu.CompilerParams(dimension_semantics=("arbitrary",),
                                             vmem_limit_bytes=_vmem_limit(est)),
        name="mix_out",
    )(oa, ob, proj, proj, x2, gate, wa, wb, wo)


def _mlp_kernel(x_ref, gain_ref, scale_ref, shift_ref, gt_ref, w1_ref, w2_ref, o_ref, h_ref, *, rows_per_iter):
    @pl.when(pl.program_id(1) == 0)
    def _():
        _norm_into(x_ref, gain_ref, scale_ref, shift_ref, h_ref, rows_per_iter)
        o_ref[...] = x_ref[...]

    hid = jnp.maximum(_dot(h_ref[...], w1_ref[...]), 0.0)
    o_ref[...] += gt_ref[0] * _dot((hid * hid).astype(BF16), w2_ref[...])


def _mlp(x2, gain, scale, shift, gate, w1, w2, seq, tm_pref=1024, th_pref=512):
    t, d = x2.shape
    hidden = w1.shape[1]
    tm = _tile(seq, tm_pref)
    th = _tile(hidden, th_pref)
    per_b = seq // tm
    est = 4 * tm * d * 4 + tm * d * 2 + 4 * d * th * 2 + tm * th * 6

    def vec():
        return pl.BlockSpec((1, 1, d), lambda i, k: (i // per_b, 0, 0))

    return pl.pallas_call(
        functools.partial(_mlp_kernel, rows_per_iter=_tile(tm, 64)),
        grid=(t // tm, hidden // th),
        in_specs=[pl.BlockSpec((tm, d), lambda i, k: (i, 0)),
                  pl.BlockSpec((1, d), lambda i, k: (0, 0)),
                  vec(), vec(), vec(),
                  pl.BlockSpec((d, th), lambda i, k: (0, k)),
                  pl.BlockSpec((th, d), lambda i, k: (k, 0))],
        out_specs=pl.BlockSpec((tm, d), lambda i, k: (i, 0)),
        out_shape=jax.ShapeDtypeStruct((t, d), F32),
        scratch_shapes=[pltpu.VMEM((tm, d), BF16)],
        compiler_params=pltpu.CompilerParams(dimension_semantics=("arbitrary", "arbitrary"),
                                             vmem_limit_bytes=_vmem_limit(est)),
        name="mlp_relu2",
    )(x2, gain, scale, shift, gate, w1, w2)


def kernel(x, c, w_ada, b_ada, norm1_gain, w_in, lb_logits, hgrn_o_gain, q_norm_gain, k_norm_gain, sinks,
           w_branch_a, w_branch_b, w_out, norm2_gain, w_mlp_in, w_mlp_out):
    bsz, seq, d = x.shape
    depth = w_ada.shape[0]
    a_w = w_branch_a.shape[1]
    b_w = w_branch_b.shape[1]
    kv_w = b_w // B_GROUP
    a0, qb0 = 0, 4 * a_w
    kb0, vb0 = qb0 + b_w, qb0 + b_w + kv_w
    ga0, gb0 = vb0 + kv_w, vb0 + kv_w + d
    n_ga0, n_gb0, n_a0, n_qb0 = 0, d, 2 * d, 2 * d + 4 * a_w
    n_kb0, n_vb0 = n_qb0 + b_w, n_qb0 + b_w + kv_w

    x2 = x.reshape(bsz * seq, d)
    for l in range(depth):
        mod = _ada(c, w_ada[l], b_ada[l])
        sh1, sc1, gt1, sh2, sc2, gt2 = (m.reshape(bsz, 1, d) for m in jnp.split(mod, N_MOD, axis=-1))
        wi = w_in[l]
        w_in_p = jnp.concatenate([wi[:, ga0:ga0 + d], wi[:, gb0:gb0 + d], wi[:, a0:a0 + 4 * a_w],
                                  wi[:, qb0:qb0 + b_w], wi[:, kb0:kb0 + kv_w], wi[:, vb0:vb0 + kv_w]],
                                 axis=1).astype(BF16)
        proj = _inp(x2, norm1_gain[l].reshape(1, d), sc1, sh1, w_in_p, seq)
        oa = _hgrn(proj, n_a0, a_w, lb_logits, hgrn_o_gain[l].reshape(1, a_w), bsz, seq, l)
        ob = _swa(proj, n_qb0, n_kb0, n_vb0, q_norm_gain[l], k_norm_gain[l], sinks[l], bsz, seq)
        x2 = _mix(oa, ob, proj, n_ga0, n_gb0, x2, gt1, w_branch_a[l].astype(BF16), w_branch_b[l].astype(BF16),
                  w_out[l].astype(BF16), seq)
        x2 = _mlp(x2, norm2_gain[l].reshape(1, d), sc2, sh2, gt2, w_mlp_in[l].astype(BF16),
                  w_mlp_out[l].astype(BF16), seq)
    return x2.reshape(bsz, seq, d)
```

```python
import functools

import jax
import jax.numpy as jnp
from jax import lax
from jax.experimental import pallas as pl
from jax.experimental.pallas import tpu as pltpu

F32 = jnp.float32
BF16 = jnp.bfloat16

EPS = 1e-6
N_MOD = 6
A_HEAD_DIM = 128
A_CHUNK = 64
B_HEAD_DIM = 64
B_GROUP = 4
WINDOW = 128
LANES = 128
V7X_VMEM_BYTES = 64 * 1024 * 1024


def _vmem_limit(estimate_bytes):
    return int(min(estimate_bytes * 5 // 4 + (4 << 20), V7X_VMEM_BYTES - (6 << 20)))


def _tile(n, pref):
    t = min(n, pref)
    while n % t:
        t -= 1
    return t


def _nt_dot(a, b):
    return lax.dot_general(a, b, (((1,), (1,)), ((), ())), preferred_element_type=F32)


def _tn_dot(a, b):
    return lax.dot_general(a, b, (((0,), (0,)), ((), ())), preferred_element_type=F32)


def _dot(a, b):
    return jnp.dot(a, b, preferred_element_type=F32)


def _sigmoid(x):
    return 0.5 * jnp.tanh(0.5 * x) + 0.5


def _silu(x):
    return x * _sigmoid(x)


def _ada_kernel(c_ref, w_ref, b_ref, o_ref):
    s = _silu(c_ref[...]).astype(BF16)
    o_ref[...] = _dot(s, w_ref[...].astype(BF16)) + b_ref[...]


def _ada(c, w, b):
    bsz, d = c.shape
    n = w.shape[1]
    rows = -(-bsz // 8) * 8
    cp = jnp.pad(c, ((0, rows - bsz), (0, 0)))
    tn = _tile(n, 1024)
    est = 2 * (d * tn * 4) + d * tn * 2 + 4 * rows * (d + 2 * tn) * 4
    out = pl.pallas_call(
        _ada_kernel,
        grid=(n // tn,),
        in_specs=[pl.BlockSpec((rows, d), lambda j: (0, 0)),
                  pl.BlockSpec((d, tn), lambda j: (0, j)),
                  pl.BlockSpec((1, tn), lambda j: (0, j))],
        out_specs=pl.BlockSpec((rows, tn), lambda j: (0, j)),
        out_shape=jax.ShapeDtypeStruct((rows, n), F32),
        compiler_params=pltpu.CompilerParams(dimension_semantics=("arbitrary",),
                                             vmem_limit_bytes=_vmem_limit(est)),
        name="ada_mod",
    )(cp, w, b.reshape(1, n))
    return out[:bsz]


def _mod_norm_rows(x, gain, scale, shift):
    y = x * lax.rsqrt(jnp.mean(x * x, axis=-1, keepdims=True) + EPS) * gain
    return y * (1.0 + scale) + shift


def _norm_into(x_ref, gain_ref, scale_ref, shift_ref, h_ref, rows_per_iter):
    tm = x_ref.shape[0]
    gain = gain_ref[...]
    scale = scale_ref[0]
    shift = shift_ref[0]

    def body(r, carry):
        rs = pl.ds(pl.multiple_of(r * rows_per_iter, rows_per_iter), rows_per_iter)
        h_ref[rs, :] = _mod_norm_rows(x_ref[rs, :], gain, scale, shift).astype(h_ref.dtype)
        return carry

    lax.fori_loop(0, tm // rows_per_iter, body, 0)


def _inp_kernel(x_ref, gain_ref, scale_ref, shift_ref, w_ref, o_ref, h_ref, *, rows_per_iter):
    @pl.when(pl.program_id(1) == 0)
    def _():
        _norm_into(x_ref, gain_ref, scale_ref, shift_ref, h_ref, rows_per_iter)

    o_ref[...] = _dot(h_ref[...], w_ref[...]).astype(o_ref.dtype)


def _inp(x2, gain, scale, shift, w, seq, tm_pref=1024, tn_pref=2432):
    t, d = x2.shape
    n = w.shape[1]
    tm = _tile(seq, tm_pref)
    tn = _tile(n, tn_pref)
    per_b = seq // tm
    est = 2 * tm * d * 4 + tm * d * 2 + 2 * d * tn * 2 + 2 * tm * tn * 2 + tm * tn * 4
    return pl.pallas_call(
        functools.partial(_inp_kernel, rows_per_iter=_tile(tm, 64)),
        grid=(t // tm, n // tn),
        in_specs=[pl.BlockSpec((tm, d), lambda i, j: (i, 0)),
                  pl.BlockSpec((1, d), lambda i, j: (0, 0)),
                  pl.BlockSpec((1, 1, d), lambda i, j: (i // per_b, 0, 0)),
                  pl.BlockSpec((1, 1, d), lambda i, j: (i // per_b, 0, 0)),
                  pl.BlockSpec((d, tn), lambda i, j: (0, j))],
        out_specs=pl.BlockSpec((tm, tn), lambda i, j: (i, j)),
        out_shape=jax.ShapeDtypeStruct((t, n), BF16),
        scratch_shapes=[pltpu.VMEM((tm, d), BF16)],
        compiler_params=pltpu.CompilerParams(dimension_semantics=("arbitrary", "arbitrary"),
                                             vmem_limit_bytes=_vmem_limit(est)),
        name="inp_proj",
    )(x2, gain, scale, shift, w)


def _split2(x):
    hi = x.astype(BF16)
    return hi, (x - hi.astype(F32)).astype(BF16)


def _hgrn_kernel(q_ref, f_ref, i_ref, g_ref, lbl_ref, gain_ref, o_ref, st_ref, *, layer, chunk, hd):
    tb, width = q_ref.shape
    n_heads = width // hd
    n_chunks = tb // chunk

    @pl.when(pl.program_id(1) == 0)
    def _():
        st_ref[...] = jnp.zeros_like(st_ref)

    lvl = lbl_ref[...]
    e = jnp.exp(lvl - jnp.max(lvl, axis=0, keepdims=True))
    lb = jnp.sum(e[:layer + 1], axis=0, keepdims=True) / jnp.sum(e, axis=0, keepdims=True)

    f = lb + (1.0 - lb) * _sigmoid(f_ref[...].astype(F32))
    r = lax.broadcasted_iota(jnp.int32, (tb, tb), 0)
    c = lax.broadcasted_iota(jnp.int32, (tb, tb), 1)
    tri = (c <= r) & (c >= (r & (-chunk)))
    tri_ones = tri.astype(BF16)
    hi, lo = _split2(jnp.log(f))
    b = _dot(tri_ones, hi) + _dot(tri_ones, lo)

    def chunk_row(ci, row):
        return b[ci * chunk + row:ci * chunk + row + 1, :]

    b_mid = jnp.concatenate([jnp.broadcast_to(chunk_row(ci, chunk // 2 - 1), (chunk, width))
                             for ci in range(n_chunks)], axis=0)
    q_dec = _silu(q_ref[...].astype(F32)) * jnp.exp(b - b_mid)
    k_dec = (1.0 - f) * jnp.exp(b_mid - b)
    q_dec16 = q_dec.astype(BF16)
    k_dec16 = k_dec.astype(BF16)
    q_int, k_st, decay = [], [], []
    for ci in range(n_chunks):
        rows = slice(ci * chunk, (ci + 1) * chunk)
        mid, last = chunk_row(ci, chunk // 2 - 1), chunk_row(ci, chunk - 1)
        q_int.append((q_dec[rows] * jnp.exp(mid)).astype(BF16))
        k_st.append((k_dec[rows] * jnp.exp(last - mid)).astype(BF16))
        decay.append(jnp.exp(last))
    v = i_ref[...]
    gate = _silu(g_ref[...].astype(F32))
    gain = gain_ref[...]

    for h in range(n_heads):
        cols = slice(h * hd, (h + 1) * hd)
        scores = jnp.where(tri, _nt_dot(q_dec16[:, cols], k_dec16[:, cols]), 0.0).astype(BF16)
        o_intra = _dot(scores, v[:, cols])
        st = st_ref[h]
        o = []
        for ci in range(n_chunks):
            rows = slice(ci * chunk, (ci + 1) * chunk)
            o.append(o_intra[rows] + _nt_dot(q_int[ci][:, cols], st.astype(BF16)))
            st = st * decay[ci][:, cols] + _tn_dot(v[rows, cols], k_st[ci][:, cols])
        st_ref[h] = st
        o = jnp.concatenate(o, axis=0)
        o = o * lax.rsqrt(jnp.mean(o * o, axis=-1, keepdims=True) + EPS) * gain[:, cols]
        o_ref[:, cols] = (o * gate[:, cols]).astype(o_ref.dtype)


def _hgrn(proj, col0, width, lb_logits, o_gain, bsz, seq, layer, tb_pref=256):
    t = proj.shape[0]
    tb = _tile(seq, tb_pref)
    per_b = seq // tb
    n_heads = width // A_HEAD_DIM
    cb = col0 // width

    def spec(k):
        return pl.BlockSpec((tb, width), lambda b, s: (b * per_b + s, cb + k))

    est = 2 * 5 * tb * width * 2 + n_heads * A_HEAD_DIM * A_HEAD_DIM * 4 + 24 * tb * width * 4
    return pl.pallas_call(
        functools.partial(_hgrn_kernel, layer=layer, chunk=A_CHUNK, hd=A_HEAD_DIM),
        grid=(bsz, per_b),
        in_specs=[spec(0), spec(1), spec(2), spec(3),
                  pl.BlockSpec(lb_logits.shape, lambda b, s: (0, 0)),
                  pl.BlockSpec((1, width), lambda b, s: (0, 0))],
        out_specs=pl.BlockSpec((tb, width), lambda b, s: (b * per_b + s, 0)),
        out_shape=jax.ShapeDtypeStruct((t, width), BF16),
        scratch_shapes=[pltpu.VMEM((n_heads, A_HEAD_DIM, A_HEAD_DIM), F32)],
        compiler_params=pltpu.CompilerParams(dimension_semantics=("arbitrary", "arbitrary"),
                                             vmem_limit_bytes=_vmem_limit(est)),
        name="hgrn2",
    )(proj, proj, proj, proj, lb_logits, o_gain)


def _group_mean_sq(x, ones_blk):
    x2 = x * x
    hi = x2.astype(BF16)
    lo = (x2 - hi.astype(F32)).astype(BF16)
    parts = []
    for g in range(x.shape[1] // LANES):
        cols = slice(g * LANES, (g + 1) * LANES)
        parts.append(_dot(hi[:, cols], ones_blk) + _dot(lo[:, cols], ones_blk))
    return jnp.concatenate(parts, axis=1) * (1.0 / B_HEAD_DIM)


def _swa_kernel(sink_ref, q_ref, kc_ref, vc_ref, kp_ref, vp_ref, qg_ref, kg_ref, o_ref):
    blk = q_ref.shape[0]
    n_pairs = q_ref.shape[1] // LANES
    has_prev = pl.program_id(1) > 0

    lane = lax.broadcasted_iota(jnp.int32, (1, LANES), 1)
    lo_half = lane < B_HEAD_DIM
    r = lax.broadcasted_iota(jnp.int32, (LANES, LANES), 0)
    c = lax.broadcasted_iota(jnp.int32, (LANES, LANES), 1)
    ones_blk = ((r < B_HEAD_DIM) == (c < B_HEAD_DIM)).astype(BF16)

    q = q_ref[...].astype(F32)
    qn = q * lax.rsqrt(_group_mean_sq(q, ones_blk) + EPS) * (qg_ref[...] * (B_HEAD_DIM ** -0.5))
    kcat = jnp.concatenate([kp_ref[...], kc_ref[...]], axis=0).astype(F32)
    kn = kcat * lax.rsqrt(_group_mean_sq(kcat, ones_blk) + EPS) * kg_ref[...]
    vcat = jnp.concatenate([vp_ref[...], vc_ref[...]], axis=0).astype(F32)

    ri = lax.broadcasted_iota(jnp.int32, (blk, blk), 0)
    ci = lax.broadcasted_iota(jnp.int32, (blk, blk), 1)
    in_cur = ci <= ri
    prev_bias = jnp.where(has_prev, 0.0, -jnp.inf)
    ones_cols = jnp.ones((2 * blk, LANES), BF16)
    pairs_per_kv = B_GROUP // 2

    def both_halves(x, in_lo):
        return jnp.where(lo_half == in_lo, x, pltpu.roll(x, B_HEAD_DIM, 1))

    for hk in range(n_pairs // pairs_per_kv):
        grp = slice((hk // 2) * LANES, (hk // 2 + 1) * LANES)
        in_lo = (hk % 2 == 0)
        keys = both_halves(kn[:, grp], in_lo).astype(BF16)
        vals = jnp.concatenate([both_halves(vcat[:, grp], in_lo).astype(BF16), ones_cols], axis=1)
        p0 = hk * pairs_per_kv
        q_rows = []
        for j in range(pairs_per_kv):
            qp = qn[:, (p0 + j) * LANES:(p0 + j + 1) * LANES]
            q_rows += [jnp.where(lo_half, qp, 0.0), jnp.where(lo_half, 0.0, qp)]
        s_all = _nt_dot(jnp.concatenate(q_rows, axis=0).astype(BF16), keys)
        e_rows, sink_terms = [], []
        for g in range(B_GROUP):
            sink = sink_ref[hk * B_GROUP + g]
            s_blk = s_all[g * blk:(g + 1) * blk]
            s = jnp.where(in_cur, s_blk[:, blk:], s_blk[:, :blk] + prev_bias)
            m = jnp.maximum(jnp.max(s, axis=-1, keepdims=True), sink)
            e = jnp.exp(s - m)
            e_rows.append(jnp.concatenate([jnp.where(in_cur, 0.0, e), jnp.where(in_cur, e, 0.0)], axis=1))
            sink_terms.append(jnp.exp(sink - m))
        pv = _dot(jnp.concatenate(e_rows, axis=0).astype(BF16), vals)
        for j in range(pairs_per_kv):
            heads = []
            for half in range(2):
                g = 2 * j + half
                rows = slice(g * blk, (g + 1) * blk)
                heads.append(pv[rows, :LANES] * (1.0 / (pv[rows, LANES:] + sink_terms[g])))
            o_ref[:, (p0 + j) * LANES:(p0 + j + 1) * LANES] = jnp.where(lo_half, heads[0], heads[1]).astype(o_ref.dtype)


def _swa(proj, q_col0, k_col0, v_col0, q_gain, k_gain, sinks, bsz, seq):
    t = proj.shape[0]
    blk = WINDOW
    nb = seq // blk
    qw = sinks.shape[0] * B_HEAD_DIM
    kvw = qw // B_GROUP
    qg = jnp.tile(q_gain, qw // B_HEAD_DIM).reshape(1, qw)
    kg = jnp.tile(k_gain, kvw // B_HEAD_DIM).reshape(1, kvw)

    def cur(col0, w):
        return pl.BlockSpec((blk, w), lambda b, n: (b * nb + n, col0 // w))

    def prev(col0, w):
        return pl.BlockSpec((blk, w), lambda b, n: (b * nb + jnp.maximum(n - 1, 0), col0 // w))

    est = 2 * (2 * blk * qw * 2 + 4 * blk * kvw * 2) + 16 * blk * qw * 4
    return pl.pallas_call(
        _swa_kernel,
        grid=(bsz, nb),
        in_specs=[pl.BlockSpec(memory_space=pltpu.SMEM),
                  cur(q_col0, qw), cur(k_col0, kvw), cur(v_col0, kvw), prev(k_col0, kvw), prev(v_col0, kvw),
                  pl.BlockSpec((1, qw), lambda b, n: (0, 0)),
                  pl.BlockSpec((1, kvw), lambda b, n: (0, 0))],
        out_specs=pl.BlockSpec((blk, qw), lambda b, n: (b * nb + n, 0)),
        out_shape=jax.ShapeDtypeStruct((t, qw), BF16),
        compiler_params=pltpu.CompilerParams(dimension_semantics=("arbitrary", "arbitrary"),
                                             vmem_limit_bytes=_vmem_limit(est)),
        name="swa_sink",
    )(sinks, proj, proj, proj, proj, proj, qg, kg)


def _mix_kernel(oa_ref, ob_ref, x_ref, gt_ref, gain_ref, scale_ref, shift_ref, wa_ref, wb_ref, wo_ref, *rest,
                n_pieces):
    ga_refs, gb_refs = rest[:n_pieces], rest[n_pieces:2 * n_pieces]
    o_ref, h_ref = rest[2 * n_pieces:]
    ya = _dot(oa_ref[...], wa_ref[...])
    yb = _dot(ob_ref[...], wb_ref[...])
    pw = ga_refs[0].shape[1]
    merged = []
    for p in range(n_pieces):
        cols = slice(p * pw, (p + 1) * pw)
        merged.append((_sigmoid(ga_refs[p][...].astype(F32)) * ya[:, cols]
                       + _sigmoid(gb_refs[p][...].astype(F32)) * yb[:, cols]).astype(BF16))
    x1 = x_ref[...] + gt_ref[0] * _dot(jnp.concatenate(merged, axis=1), wo_ref[...])
    o_ref[...] = x1
    h_ref[...] = _mod_norm_rows(x1, gain_ref[...], scale_ref[0], shift_ref[0]).astype(h_ref.dtype)


def _mix(oa, ob, proj, ga_col0, gb_col0, x2, gate, gain2, scale2, shift2, wa, wb, wo, seq, tm_pref=512):
    t, d = x2.shape
    tm = _tile(seq, tm_pref)
    per_b = seq // tm
    wa_w, wb_w = wa.shape[0], wb.shape[0]
    pw = d
    while ga_col0 % pw or gb_col0 % pw:
        pw //= 2
    assert pw % LANES == 0
    n_pieces = d // pw
    est = (2 * tm * (wa_w + wb_w + 2 * d) * 2 + 4 * tm * d * 4 + 2 * tm * d * 2 + (wa_w + wb_w + d) * d * 2
           + 5 * tm * d * 4)
    resident = dict(pipeline_mode=pl.Buffered(1))

    def vec():
        return pl.BlockSpec((1, 1, d), lambda i: (i // per_b, 0, 0))

    def pieces(col0):
        return [pl.BlockSpec((tm, pw), lambda i, p=p: (i, col0 // pw + p)) for p in range(n_pieces)]

    return pl.pallas_call(
        functools.partial(_mix_kernel, n_pieces=n_pieces),
        grid=(t // tm,),
        in_specs=[pl.BlockSpec((tm, wa_w), lambda i: (i, 0)),
                  pl.BlockSpec((tm, wb_w), lambda i: (i, 0)),
                  pl.BlockSpec((tm, d), lambda i: (i, 0)),
                  vec(),
                  pl.BlockSpec((1, d), lambda i: (0, 0)),
                  vec(), vec(),
                  pl.BlockSpec((wa_w, d), lambda i: (0, 0), **resident),
                  pl.BlockSpec((wb_w, d), lambda i: (0, 0), **resident),
                  pl.BlockSpec((d, d), lambda i: (0, 0), **resident)] + pieces(ga_col0) + pieces(gb_col0),
        out_specs=[pl.BlockSpec((tm, d), lambda i: (i, 0)), pl.BlockSpec((tm, d), lambda i: (i, 0))],
        out_shape=[jax.ShapeDtypeStruct((t, d), F32), jax.ShapeDtypeStruct((t, d), BF16)],
        compiler_params=pltpu.CompilerParams(dimension_semantics=("arbitrary",),
                                             vmem_limit_bytes=_vmem_limit(est)),
        name="mix_out",
    )(oa, ob, x2, gate, gain2, scale2, shift2, wa, wb, wo, *([proj] * (2 * n_pieces)))


def _mlp_kernel(x_ref, h_ref, gt_ref, w1_ref, w2_ref, o_ref):
    @pl.when(pl.program_id(1) == 0)
    def _():
        o_ref[...] = x_ref[...]

    hid = jnp.maximum(_dot(h_ref[...], w1_ref[...]), 0.0)
    o_ref[...] += gt_ref[0] * _dot((hid * hid).astype(BF16), w2_ref[...])


def _mlp(x2, h2, gate, w1, w2, seq, tm_pref=1024, th_pref=512):
    t, d = x2.shape
    hidden = w1.shape[1]
    tm = _tile(seq, tm_pref)
    th = _tile(hidden, th_pref)
    per_b = seq // tm
    est = 4 * tm * d * 4 + 2 * tm * d * 2 + 4 * d * th * 2 + tm * th * 6
    return pl.pallas_call(
        _mlp_kernel,
        grid=(t // tm, hidden // th),
        in_specs=[pl.BlockSpec((tm, d), lambda i, k: (i, 0)),
                  pl.BlockSpec((tm, d), lambda i, k: (i, 0)),
                  pl.BlockSpec((1, 1, d), lambda i, k: (i // per_b, 0, 0)),
                  pl.BlockSpec((d, th), lambda i, k: (0, k)),
                  pl.BlockSpec((th, d), lambda i, k: (k, 0))],
        out_specs=pl.BlockSpec((tm, d), lambda i, k: (i, 0)),
        out_shape=jax.ShapeDtypeStruct((t, d), F32),
        compiler_params=pltpu.CompilerParams(dimension_semantics=("arbitrary", "arbitrary"),
                                             vmem_limit_bytes=_vmem_limit(est)),
        name="mlp_relu2",
    )(x2, h2, gate, w1, w2)


def kernel(x, c, w_ada, b_ada, norm1_gain, w_in, lb_logits, hgrn_o_gain, q_norm_gain, k_norm_gain, sinks,
           w_branch_a, w_branch_b, w_out, norm2_gain, w_mlp_in, w_mlp_out):
    bsz, seq, d = x.shape
    depth = w_ada.shape[0]
    a_w = w_branch_a.shape[1]
    b_w = w_branch_b.shape[1]
    kv_w = b_w // B_GROUP
    a0, qb0 = 0, 4 * a_w
    kb0, vb0 = qb0 + b_w, qb0 + b_w + kv_w
    ga0, gb0 = vb0 + kv_w, vb0 + kv_w + d
    assert a0 % a_w == 0 and qb0 % b_w == 0 and kb0 % kv_w == 0 and vb0 % kv_w == 0

    x2 = x.reshape(bsz * seq, d)
    for l in range(depth):
        mod = _ada(c, w_ada[l], b_ada[l])
        sh1, sc1, gt1, sh2, sc2, gt2 = (m.reshape(bsz, 1, d) for m in jnp.split(mod, N_MOD, axis=-1))
        proj = _inp(x2, norm1_gain[l].reshape(1, d), sc1, sh1, w_in[l].astype(BF16), seq)
        oa = _hgrn(proj, a0, a_w, lb_logits, hgrn_o_gain[l].reshape(1, a_w), bsz, seq, l)
        ob = _swa(proj, qb0, kb0, vb0, q_norm_gain[l], k_norm_gain[l], sinks[l], bsz, seq)
        x2, h2 = _mix(oa, ob, proj, ga0, gb0, x2, gt1, norm2_gain[l].reshape(1, d), sc2, sh2,
                      w_branch_a[l].astype(BF16), w_branch_b[l].astype(BF16), w_out[l].astype(BF16), seq)
        x2 = _mlp(x2, h2, gt2, w_mlp_in[l].astype(BF16), w_mlp_out[l].astype(BF16), seq)
    return x2.reshape(bsz, seq, d)
```
